```python
import math
import jax, jax.numpy as jnp
from jax import lax
import numpy as np

D_MODEL = 1024
BATCH = 8
SEQ = 4096
DEPTH = 2

ATT_HEADS = 8
HEAD_DIM = 64
ATT_WIDTH = ATT_HEADS * 2 * HEAD_DIM
Q_BLOCK = 128
ROPE_THETA = 10000.0
MAX_POS_OFFSET = 1024
ML_HEADS = 8
ML_QK_DIM = D_MODEL // 16
ML_V_DIM = D_MODEL // 8
ML_QK = ML_HEADS * ML_QK_DIM
ML_V = ML_HEADS * ML_V_DIM
ML_PROJ = 2 * ML_QK + 2 * ML_V + 2 * ML_HEADS
ML_CHUNK = 64
GATE_SOFTCAP = 15.0
N_EXPERTS = 32
TOP_K = 4
D_FF = D_MODEL
SWIGLU_ALPHA = 1.702
SWIGLU_LIMIT = 7.0
EXPERT_BLOCK = 512
RMS_EPS = 1e-6
N_MIXERS = 2
N_ATT_LAYERS = (DEPTH + 1) // 2
N_ML_LAYERS = DEPTH // 2

kernel_name = 'hybrid_diffattn_mlstm_moe_adaln'


def rms_norm(x, gain):
    xf = x.astype(jnp.float32)
    y = xf * lax.rsqrt(jnp.mean(xf * xf, axis=-1, keepdims=True) + RMS_EPS)
    return (y * gain.astype(jnp.float32)).astype(x.dtype)


def rope_tables(positions):
    inv_freq = ROPE_THETA ** (-jnp.arange(0, HEAD_DIM, 2, dtype=jnp.float32) / HEAD_DIM)
    ang = positions.astype(jnp.float32)[..., None] * inv_freq
    return jnp.cos(ang), jnp.sin(ang)


def apply_rope(x, cos, sin):
    xf = x.astype(jnp.float32)
    x1, x2 = jnp.split(xf, 2, axis=-1)
    return jnp.concatenate([x1 * cos - x2 * sin, x2 * cos + x1 * sin], axis=-1).astype(x.dtype)


def softcap(z):
    return GATE_SOFTCAP * jnp.tanh(z / GATE_SOFTCAP)


def diff_attention(h, cos, sin, w_in, w_out, q_gain, k_gain, lam_q1, lam_k1, lam_q2, lam_k2, sub_gain, lambda_init):
    B, S, _ = h.shape
    f32 = jnp.float32
    nq = S // Q_BLOCK
    q, k, v = jnp.split(h @ w_in, 3, axis=-1)
    q = q.reshape(B, S, ATT_HEADS, 2, HEAD_DIM)
    k = k.reshape(B, S, ATT_HEADS, 2, HEAD_DIM)
    v = v.reshape(B, S, ATT_HEADS, 2 * HEAD_DIM)
    cs, sn = cos[:, :, None, None, :], sin[:, :, None, None, :]
    q = apply_rope(rms_norm(q, q_gain), cs, sn)
    k = apply_rope(rms_norm(k, k_gain), cs, sn)
    lam = (jnp.exp(jnp.sum(lam_q1.astype(f32) * lam_k1.astype(f32)))
           - jnp.exp(jnp.sum(lam_q2.astype(f32) * lam_k2.astype(f32))) + lambda_init)
    q_blocks = q.reshape(B, nq, Q_BLOCK, ATT_HEADS, 2, HEAD_DIM).transpose(1, 0, 3, 4, 2, 5)
    k_t = k.transpose(0, 2, 3, 1, 4)
    v_t = v.transpose(0, 2, 1, 3)
    key_pos = jnp.arange(S)
    scale = HEAD_DIM ** -0.5

    def attend_block(args):
        q_blk, blk = args
        s = jnp.einsum('bhcqd,bhckd->bhcqk', q_blk, k_t).astype(f32) * scale
        q_pos = blk * Q_BLOCK + jnp.arange(Q_BLOCK)
        s = jnp.where(key_pos[None, :] <= q_pos[:, None], s, -jnp.inf)
        p = jax.nn.softmax(s, axis=-1)
        a = p[:, :, 0] - lam * p[:, :, 1]
        return jnp.einsum('bhqk,bhkd->bhqd', a.astype(v_t.dtype), v_t)

    o = lax.map(attend_block, (q_blocks, jnp.arange(nq)))
    o = o.transpose(1, 0, 3, 2, 4).reshape(B, S, ATT_HEADS, 2 * HEAD_DIM)
    o = rms_norm(o, sub_gain) * (1.0 - lambda_init)
    return o.reshape(B, S, ATT_WIDTH) @ w_out


def mlstm_chunkwise(q, k, v, i_pre, log_f):
    f32 = jnp.float32
    B, H, S, dk = q.shape
    dv = v.shape[-1]
    L = ML_CHUNK
    NC = S // L

    def to_chunks(a):
        a = a.astype(f32).reshape(a.shape[:2] + (NC, L) + a.shape[3:])
        return jnp.moveaxis(a, 2, 0)

    xs = tuple(to_chunks(a) for a in (q, k, v, i_pre, log_f))
    causal = jnp.tril(jnp.ones((L, L), dtype=bool))

    def step(carry, inp):
        C, n, m = carry
        qb, kb, vb, ib, fb = inp
        b = jnp.cumsum(fb, axis=-1)
        g = b[..., -1]
        log_d = b[..., :, None] - b[..., None, :] + ib[..., None, :]
        log_d = jnp.where(causal, log_d, -jnp.inf)
        inter = b + m[..., None]
        m_row = jnp.maximum(inter, jnp.max(log_d, axis=-1))
        s = jnp.einsum('bhld,bhsd->bhls', qb, kb) * jnp.exp(log_d - m_row[..., None])
        w_inter = jnp.exp(inter - m_row)
        num = (w_inter[..., None] * jnp.einsum('bhld,bhdv->bhlv', qb, C)
               + jnp.einsum('bhls,bhsv->bhlv', s, vb))
        den = w_inter * jnp.einsum('bhld,bhd->bhl', qb, n) + jnp.sum(s, axis=-1)
        h = num / jnp.maximum(jnp.abs(den), jnp.exp(-m_row))[..., None]
        log_w = g[..., None] - b + ib
        m_new = jnp.maximum(g + m, jnp.max(log_w, axis=-1))
        ws = jnp.exp(log_w - m_new[..., None])
        decay = jnp.exp(g + m - m_new)
        C_new = decay[..., None, None] * C + jnp.einsum('bhl,bhld,bhlv->bhdv', ws, kb, vb)
        n_new = decay[..., None] * n + jnp.einsum('bhl,bhld->bhd', ws, kb)
        return (C_new, n_new, m_new), h

    init = (jnp.zeros((B, H, dk, dv), f32), jnp.zeros((B, H, dk), f32), jnp.zeros((B, H), f32))
    _, hs = lax.scan(step, init, xs)
    return jnp.moveaxis(hs, 0, 2).reshape(B, H, S, dv)


def mlstm_mixer(h, w_in, b_igate, b_fgate, out_gain, w_out):
    B, S, _ = h.shape
    f32 = jnp.float32
    splits = [ML_QK, 2 * ML_QK, 2 * ML_QK + ML_V, 2 * ML_QK + 2 * ML_V, 2 * ML_QK + 2 * ML_V + ML_HEADS]
    q, k, v, o, i_pre, f_pre = jnp.split(h @ w_in, splits, axis=-1)
    q = q.reshape(B, S, ML_HEADS, ML_QK_DIM).transpose(0, 2, 1, 3)
    k = (k * (ML_QK_DIM ** -0.5)).reshape(B, S, ML_HEADS, ML_QK_DIM).transpose(0, 2, 1, 3)
    v = v.reshape(B, S, ML_HEADS, ML_V_DIM).transpose(0, 2, 1, 3)
    i_pre = softcap(i_pre.astype(f32) + b_igate.astype(f32)).transpose(0, 2, 1)
    log_f = jax.nn.log_sigmoid(softcap(f_pre.astype(f32) + b_fgate.astype(f32))).transpose(0, 2, 1)
    ht = mlstm_chunkwise(q, k, v, i_pre, log_f).transpose(0, 2, 1, 3)
    ht = rms_norm(ht, out_gain.reshape(ML_HEADS, ML_V_DIM)).astype(h.dtype)
    y = jax.nn.sigmoid(o).reshape(B, S, ML_HEADS, ML_V_DIM) * ht
    return y.reshape(B, S, ML_V) @ w_out


def moe_ffn(h, w_router, b_router, w_gate_up, b_gate_up, w_down, b_down):
    B, S, D = h.shape
    T = B * S
    G = EXPERT_BLOCK
    xt = h.reshape(T, D)
    logits = (xt @ w_router).astype(jnp.float32) + b_router.astype(jnp.float32)
    top_val, top_idx = lax.top_k(logits, TOP_K)
    gates = jax.nn.softmax(top_val, axis=-1)
    A = T * TOP_K
    flat_e = top_idx.reshape(A).astype(jnp.int32)
    flat_t = (jnp.arange(A, dtype=jnp.int32) // TOP_K)
    flat_w = gates.reshape(A)
    order = jnp.argsort(flat_e)
    se, st, sw = flat_e[order], flat_t[order], flat_w[order]
    counts = jnp.bincount(flat_e, length=N_EXPERTS)
    start = jnp.cumsum(counts) - counts
    padded = (counts + G - 1) // G * G
    pend = jnp.cumsum(padded)
    pstart = pend - padded
    slot = pstart[se] + (jnp.arange(A) - start[se])
    n_blocks = -(-A // G) + N_EXPERTS
    P = n_blocks * G
    slot_tok = jnp.full((P,), T, dtype=jnp.int32).at[slot].set(st)
    slot_w = jnp.zeros((P,), h.dtype).at[slot].set(sw.astype(h.dtype))
    block_e = jnp.minimum(jnp.searchsorted(pend, jnp.arange(n_blocks) * G, side='right'), N_EXPERTS - 1)
    xpad = jnp.concatenate([xt, jnp.zeros((1, D), xt.dtype)], axis=0)

    def expert_block(args):
        idx, e = args
        xb = xpad[idx]
        gu = xb @ w_gate_up[e] + b_gate_up[e]
        glu = jnp.minimum(gu[:, ::2], SWIGLU_LIMIT)
        lin = jnp.clip(gu[:, 1::2], -SWIGLU_LIMIT, SWIGLU_LIMIT)
        act = glu * jax.nn.sigmoid(SWIGLU_ALPHA * glu) * (lin + 1.0)
        return act @ w_down[e] + b_down[e]

    ys = lax.map(expert_block, (slot_tok.reshape(n_blocks, G), block_e)).reshape(P, D)
    out = jax.ops.segment_sum(ys * slot_w[:, None], slot_tok, num_segments=T + 1)[:T]
    return out.reshape(B, S, D)


def setup_inputs(seed: int = 0) -> dict:
    key = jax.random.key(seed)
    ks = jax.random.split(key, 32)
    D = D_MODEL
    f32 = jnp.float32

    def nrm(k, shape, scale):
        return jax.random.normal(k, shape, f32) * scale

    na, nm = N_ATT_LAYERS, N_ML_LAYERS
    positions = (jax.random.randint(ks[2], (BATCH, 1), 0, MAX_POS_OFFSET, dtype=jnp.int32)
                 + jnp.arange(SEQ, dtype=jnp.int32)[None, :])
    return {
        'x': nrm(ks[0], (BATCH, SEQ, D), 1.0),
        'c': nrm(ks[1], (BATCH, D), 1.0),
        'positions': positions,
        'ada_w': nrm(ks[3], (DEPTH, D, 6 * D), 0.5 * D ** -0.5),
        'ada_b': nrm(ks[4], (DEPTH, 6 * D), 0.02),
        'mix_norm': 1.0 + nrm(ks[5], (DEPTH, D), 0.02),
        'ffn_norm': 1.0 + nrm(ks[6], (DEPTH, D), 0.02),
        'att_w_in': nrm(ks[7], (na, D, 3 * ATT_WIDTH), D ** -0.5),
        'att_w_out': nrm(ks[8], (na, ATT_WIDTH, D), ATT_WIDTH ** -0.5),
        'att_q_norm': 1.0 + nrm(ks[9], (na, HEAD_DIM), 0.02),
        'att_k_norm': 1.0 + nrm(ks[10], (na, HEAD_DIM), 0.02),
        'att_lam_q1': nrm(ks[11], (na, HEAD_DIM), 0.1),
        'att_lam_k1': nrm(ks[12], (na, HEAD_DIM), 0.1),
        'att_lam_q2': nrm(ks[13], (na, HEAD_DIM), 0.1),
        'att_lam_k2': nrm(ks[14], (na, HEAD_DIM), 0.1),
        'att_sub_norm': 1.0 + nrm(ks[15], (na, 2 * HEAD_DIM), 0.02),
        'ml_w_in': nrm(ks[16], (nm, D, ML_PROJ), D ** -0.5),
        'ml_b_igate': nrm(ks[17], (nm, ML_HEADS), 0.1),
        'ml_b_fgate': jnp.linspace(3.0, 6.0, ML_HEADS, dtype=f32)[None, :] + nrm(ks[18], (nm, ML_HEADS), 0.1),
        'ml_out_norm': 1.0 + nrm(ks[19], (nm, ML_V), 0.02),
        'ml_w_out': nrm(ks[20], (nm, ML_V, D), ML_V ** -0.5),
        'router_w': nrm(ks[21], (DEPTH, D, N_EXPERTS), D ** -0.5),
        'router_b': nrm(ks[22], (DEPTH, N_EXPERTS), 0.01),
        'moe_w_gate_up': nrm(ks[23], (DEPTH, N_EXPERTS, D, 2 * D_FF), D ** -0.5),
        'moe_b_gate_up': nrm(ks[24], (DEPTH, N_EXPERTS, 2 * D_FF), 0.02),
        'moe_w_down': nrm(ks[25], (DEPTH, N_EXPERTS, D_FF, D), D_FF ** -0.5),
        'moe_b_down': nrm(ks[26], (DEPTH, N_EXPERTS, D), 0.02),
    }


def reference(x, c, positions, ada_w, ada_b, mix_norm, ffn_norm,
              att_w_in, att_w_out, att_q_norm, att_k_norm, att_lam_q1, att_lam_k1, att_lam_q2, att_lam_k2, att_sub_norm,
              ml_w_in, ml_b_igate, ml_b_fgate, ml_out_norm, ml_w_out,
              router_w, router_b, moe_w_gate_up, moe_b_gate_up, moe_w_down, moe_b_down):
    cos, sin = rope_tables(positions)
    c_act = jax.nn.silu(c)
    for layer in range(DEPTH):
        mod = (c_act @ ada_w[layer] + ada_b[layer])[:, None, :]
        sh1, sc1, g1, sh2, sc2, g2 = jnp.split(mod, 6, axis=-1)
        h = rms_norm(x, mix_norm[layer]) * (1.0 + sc1) + sh1
        j = layer // N_MIXERS
        if layer % N_MIXERS == 0:
            lambda_init = 0.8 - 0.6 * math.exp(-0.3 * layer)
            y = diff_attention(h, cos, sin, att_w_in[j], att_w_out[j], att_q_norm[j], att_k_norm[j],
                               att_lam_q1[j], att_lam_k1[j], att_lam_q2[j], att_lam_k2[j], att_sub_norm[j], lambda_init)
        else:
            y = mlstm_mixer(h, ml_w_in[j], ml_b_igate[j], ml_b_fgate[j], ml_out_norm[j], ml_w_out[j])
        x = x + g1 * y
        h = rms_norm(x, ffn_norm[layer]) * (1.0 + sc2) + sh2
        x = x + g2 * moe_ffn(h, router_w[layer], router_b[layer], moe_w_gate_up[layer],
                             moe_b_gate_up[layer], moe_w_down[layer], moe_b_down[layer])
    return x
```

```python
import functools
import math

import jax
import jax.numpy as jnp
from jax import lax
from jax.experimental import pallas as pl
from jax.experimental.pallas import tpu as pltpu

F32 = jnp.float32
BF16 = jnp.bfloat16

HEAD_DIM = 64
ATT_HEADS = 8
ML_HEADS = 8
ML_QK_DIM = 64
ML_V_DIM = 128
ROPE_THETA = 10000.0
GATE_SOFTCAP = 15.0
N_EXPERTS = 32
TOP_K = 4
SWIGLU_ALPHA = 1.702
SWIGLU_LIMIT = 7.0
EXPERT_BLOCK = 512
RMS_EPS = 1e-6
NEG_BIG = -1e30
LANES = 128

VMEM_LIMIT = 56 * 1024 * 1024


def _cparams(sem):
    return pltpu.CompilerParams(dimension_semantics=sem, vmem_limit_bytes=VMEM_LIMIT)


def _norm_mod(x, gain, sc, sh):
    ms = jnp.mean(x * x, axis=-1, keepdims=True)
    return (x * lax.rsqrt(ms + RMS_EPS) * gain) * (1.0 + sc) + sh


def _adaln_kernel(c_ref, w_ref, b_ref, o_ref):
    c = c_ref[...]
    ca = c * jax.nn.sigmoid(c)
    o_ref[0] = jnp.dot(ca, w_ref[0], preferred_element_type=F32,
                       precision=lax.Precision.HIGHEST) + b_ref[0]


def _adaln(c, ada_w, ada_b):
    depth, d, n = ada_w.shape
    b = c.shape[0]
    tn = n // 4
    return pl.pallas_call(
        _adaln_kernel,
        grid=(depth, n // tn),
        in_specs=[pl.BlockSpec((b, d), lambda l, j: (0, 0)),
                  pl.BlockSpec((1, d, tn), lambda l, j: (l, 0, j)),
                  pl.BlockSpec((1, 1, tn), lambda l, j: (l, 0, j))],
        out_specs=pl.BlockSpec((1, b, tn), lambda l, j: (l, 0, j)),
        out_shape=jax.ShapeDtypeStruct((depth, b, n), F32),
        compiler_params=_cparams(("arbitrary", "arbitrary")),
        name="adaln",
    )(c, ada_w, ada_b.reshape(depth, 1, n))


def _att_inproj_kernel(x_ref, gain_ref, sc_ref, sh_ref, w_ref, g64_ref, qg_ref, kg_ref, cos_ref, sin_ref,
                       q_ref, k_ref, v_ref):
    ts = x_ref.shape[1]
    width = q_ref.shape[2]
    h = _norm_mod(x_ref[0], gain_ref[...], sc_ref[0], sh_ref[0]).astype(BF16)
    cos = cos_ref[0]
    sin = sin_ref[0]
    lane = lax.broadcasted_iota(jnp.int32, (ts, LANES), 1)
    first_half = (lane % HEAD_DIM) < (HEAD_DIM // 2)
    cw = 2 * LANES

    def qk_part(col0, g_ref, out_ref):
        for c in range(width // cw):
            z = jnp.dot(h, w_ref[:, col0 + c * cw: col0 + (c + 1) * cw], preferred_element_type=F32)
            msq = jnp.dot((z * z).astype(BF16), g64_ref[...], preferred_element_type=F32)
            zn = z * lax.rsqrt(msq + RMS_EPS) * g_ref[...]
            for s in range(cw // LANES):
                zz = zn[:, s * LANES:(s + 1) * LANES]
                sw = jnp.where(first_half, pltpu.roll(zz, LANES - HEAD_DIM // 2, 1),
                               pltpu.roll(zz, HEAD_DIM // 2, 1))
                lo = c * cw + s * LANES
                out_ref[0, :, lo:lo + LANES] = (zz * cos + sw * sin).astype(BF16)

    qk_part(0, qg_ref, q_ref)
    qk_part(width, kg_ref, k_ref)
    for c in range(width // cw):
        lo = 2 * width + c * cw
        v_ref[0, :, c * cw:(c + 1) * cw] = jnp.dot(
            h, w_ref[:, lo:lo + cw], preferred_element_type=F32).astype(BF16)


def _att_inproj(x, gain, sc, sh, w_bf, q_gain, k_gain, cos_t, sin_t, ts):
    b, s, d = x.shape
    width = w_bf.shape[1] // 3
    cw = 2 * LANES
    gid = jnp.arange(cw) // HEAD_DIM
    g64 = jnp.where(gid[:, None] == gid[None, :], 1.0 / HEAD_DIM, 0.0).astype(BF16)
    reps = cw // HEAD_DIM
    qg = jnp.tile(q_gain.astype(F32) * (HEAD_DIM ** -0.5), reps).reshape(1, cw)
    kg = jnp.tile(k_gain.astype(F32), reps).reshape(1, cw)
    row = lambda i, j: (i, j, 0)
    per_b = lambda i, j: (i, 0, 0)
    const2 = lambda i, j: (0, 0)
    out = jax.ShapeDtypeStruct((b, s, width), BF16)
    return pl.pallas_call(
        _att_inproj_kernel,
        grid=(b, s // ts),
        in_specs=[pl.BlockSpec((1, ts, d), row),
                  pl.BlockSpec((1, d), const2),
                  pl.BlockSpec((1, 1, d), per_b),
                  pl.BlockSpec((1, 1, d), per_b),
                  pl.BlockSpec(w_bf.shape, const2),
                  pl.BlockSpec((cw, cw), const2),
                  pl.BlockSpec((1, cw), const2),
                  pl.BlockSpec((1, cw), const2),
                  pl.BlockSpec((1, ts, LANES), row),
                  pl.BlockSpec((1, ts, LANES), row)],
        out_specs=[pl.BlockSpec((1, ts, width), row)] * 3,
        out_shape=[out, out, out],
        compiler_params=_cparams(("parallel", "arbitrary")),
        name="att_inproj",
    )(x, gain.reshape(1, d), sc, sh, w_bf, g64, qg, kg, cos_t, sin_t)


def _attn_kernel(q_ref, k_ref, v_ref, lam_ref, sg_ref, o_ref, m_sc, l_sc, acc_sc, *, tq, lambda_init):
    iq = pl.program_id(2)
    q = q_ref[0]
    lane = lax.broadcasted_iota(jnp.int32, q.shape, 1)
    zero = jnp.zeros_like(q)
    qs = jnp.concatenate([jnp.where(lane < HEAD_DIM, q, zero), jnp.where(lane >= HEAD_DIM, q, zero)], axis=0)
    nt = (((1,), (1,)), ((), ()))

    start = pl.multiple_of(iq * tq, tq)
    kd = k_ref[0, pl.ds(start, tq), :]
    vd = v_ref[0, pl.ds(start, tq), :]
    s = lax.dot_general(qs, kd, nt, preferred_element_type=F32)
    r = lax.broadcasted_iota(jnp.int32, s.shape, 0)
    col = lax.broadcasted_iota(jnp.int32, s.shape, 1)
    rq = jnp.where(r >= tq, r - tq, r)
    s = jnp.where(col <= rq, s, NEG_BIG)
    m0 = jnp.max(s, axis=-1, keepdims=True)
    p = jnp.exp(s - m0)
    m_sc[...] = m0
    l_sc[...] = jnp.sum(p, axis=-1, keepdims=True)
    acc_sc[...] = jnp.dot(p.astype(BF16), vd, preferred_element_type=F32)

    def body(j, carry):
        off = pl.multiple_of(j * tq, tq)
        kj = k_ref[0, pl.ds(off, tq), :]
        vj = v_ref[0, pl.ds(off, tq), :]
        sj = lax.dot_general(qs, kj, nt, preferred_element_type=F32)
        m_prev = m_sc[...]
        m_new = jnp.maximum(m_prev, jnp.max(sj, axis=-1, keepdims=True))
        alpha = jnp.exp(m_prev - m_new)
        pj = jnp.exp(sj - m_new)
        l_sc[...] = alpha * l_sc[...] + jnp.sum(pj, axis=-1, keepdims=True)
        acc_sc[...] = alpha * acc_sc[...] + jnp.dot(pj.astype(BF16), vj, preferred_element_type=F32)
        m_sc[...] = m_new
        return carry

    lax.fori_loop(0, iq, body, 0)

    acc = acc_sc[...]
    l = l_sc[...]
    lv = lam_ref[...]
    lam = (jnp.exp(jnp.sum(lv[0:1] * lv[1:2], axis=-1, keepdims=True))
           - jnp.exp(jnp.sum(lv[2:3] * lv[3:4], axis=-1, keepdims=True)) + lambda_init)
    o = acc[:tq] / l[:tq] - lam * (acc[tq:] / l[tq:])
    ms = jnp.mean(o * o, axis=-1, keepdims=True)
    o = o * lax.rsqrt(ms + RMS_EPS) * sg_ref[...] * (1.0 - lambda_init)
    o_ref[0] = o.astype(BF16)


def _diff_attention(q, k, v, lam_vecs, sub_gain, lambda_init, tq):
    b, s, width = q.shape
    nh = width // LANES
    kern = functools.partial(_attn_kernel, tq=tq, lambda_init=lambda_init)
    return pl.pallas_call(
        kern,
        grid=(b, nh, s // tq),
        in_specs=[pl.BlockSpec((1, tq, LANES), lambda i, h, j: (i, j, h)),
                  pl.BlockSpec((1, s, LANES), lambda i, h, j: (i, 0, h)),
                  pl.BlockSpec((1, s, LANES), lambda i, h, j: (i, 0, h)),
                  pl.BlockSpec(lam_vecs.shape, lambda i, h, j: (0, 0)),
                  pl.BlockSpec((1, LANES), lambda i, h, j: (0, 0))],
        out_specs=pl.BlockSpec((1, tq, LANES), lambda i, h, j: (i, j, h)),
        out_shape=jax.ShapeDtypeStruct((b, s, width), BF16),
        scratch_shapes=[pltpu.VMEM((2 * tq, 1), F32), pltpu.VMEM((2 * tq, 1), F32),
                        pltpu.VMEM((2 * tq, LANES), F32)],
        compiler_params=_cparams(("parallel", "parallel", "arbitrary")),
        name="diff_attn",
    )(q, k, v, lam_vecs, sub_gain.reshape(1, LANES))


def _outproj_kernel(o_ref, w_ref, x_ref, g1_ref, gain_ref, sc_ref, sh_ref, rw_ref, rb_ref,
                    xo_ref, h2_ref, eidx_ref, gate_ref):
    y = jnp.dot(o_ref[0], w_ref[...], preferred_element_type=F32)
    xn = x_ref[0] + g1_ref[0] * y
    xo_ref[0] = xn
    h2 = _norm_mod(xn, gain_ref[...], sc_ref[0], sh_ref[0])
    h2_ref[0] = h2
    logits = jnp.dot(h2.astype(BF16), rw_ref[...], preferred_element_type=F32) + rb_ref[...]
    lane = lax.broadcasted_iota(jnp.int32, logits.shape, 1).astype(F32)
    vals, idxs = [], []
    for _ in range(TOP_K):
        mx = jnp.max(logits, axis=-1, keepdims=True)
        ix = jnp.min(jnp.where(logits == mx, lane, float(LANES)), axis=-1, keepdims=True)
        vals.append(mx)
        idxs.append(ix)
        logits = jnp.where(lane == ix, NEG_BIG * 2.0, logits)
    es = [jnp.exp(vk - vals[0]) for vk in vals]
    den = es[0] + es[1] + es[2] + es[3]
    eo = jnp.zeros(lane.shape, F32)
    go = jnp.zeros(lane.shape, F32)
    for kk in range(TOP_K):
        eo = jnp.where(lane == float(kk), idxs[kk], eo)
        go = jnp.where(lane == float(kk), es[kk] / den, go)
    eidx_ref[0] = eo.astype(jnp.int32)
    gate_ref[0] = go


def _outproj(o, w_bf, x, g1, gain, sc, sh, rw_pad, rb_pad, ts):
    b, s, d = x.shape
    row = lambda i, j: (i, j, 0)
    per_b = lambda i, j: (i, 0, 0)
    const2 = lambda i, j: (0, 0)
    return pl.pallas_call(
        _outproj_kernel,
        grid=(b, s // ts),
        in_specs=[pl.BlockSpec((1, ts, o.shape[2]), row),
                  pl.BlockSpec(w_bf.shape, const2),
                  pl.BlockSpec((1, ts, d), row),
                  pl.BlockSpec((1, 1, d), per_b),
                  pl.BlockSpec((1, d), const2),
                  pl.BlockSpec((1, 1, d), per_b),
                  pl.BlockSpec((1, 1, d), per_b),
                  pl.BlockSpec(rw_pad.shape, const2),
                  pl.BlockSpec((1, LANES), const2)],
        out_specs=[pl.BlockSpec((1, ts, d), row), pl.BlockSpec((1, ts, d), row),
                   pl.BlockSpec((1, ts, LANES), row), pl.BlockSpec((1, ts, LANES), row)],
        out_shape=[jax.ShapeDtypeStruct((b, s, d), F32), jax.ShapeDtypeStruct((b, s, d), F32),
                   jax.ShapeDtypeStruct((b, s, LANES), jnp.int32), jax.ShapeDtypeStruct((b, s, LANES), F32)],
        compiler_params=_cparams(("parallel", "arbitrary")),
        name="outproj_router",
    )(o, w_bf, x, g1, gain.reshape(1, d), sc, sh, rw_pad, rb_pad)


def _moe_kernel(be_ref, src_ref, srcn_ref, dst_ref, sw_ref, h_hbm, wg_ref, wl_ref, bg_ref, bl_ref, wd_ref, bd_ref,
                y_hbm, xbuf, ybuf, gsem, ssem):
    del be_ref
    i = pl.program_id(0)
    n = pl.num_programs(0)
    g = xbuf.shape[1]
    slot = i % 2

    def issue_gather(idx_ref, sl):
        def body(r, c):
            tok = idx_ref[0, 0, r]
            pltpu.make_async_copy(h_hbm.at[pl.ds(tok, 1)], xbuf.at[sl, pl.ds(r, 1)], gsem.at[sl]).start()
            return c
        lax.fori_loop(0, g, body, 0, unroll=8)

    def wait_gather(sl):
        pltpu.make_async_copy(h_hbm.at[pl.ds(0, g)], xbuf.at[sl], gsem.at[sl]).wait()

    def wait_scatter(sl):
        pltpu.make_async_copy(ybuf.at[sl], y_hbm.at[pl.ds(0, g)], ssem.at[sl]).wait()

    @pl.when(i == 0)
    def _():
        issue_gather(src_ref, 0)

    @pl.when(i + 1 < n)
    def _():
        issue_gather(srcn_ref, 1 - slot)

    wait_gather(slot)
    x = xbuf[slot].astype(BF16)
    glu = jnp.dot(x, wg_ref[0], preferred_element_type=F32) + bg_ref[0]
    lin = jnp.dot(x, wl_ref[0], preferred_element_type=F32) + bl_ref[0]
    glu = jnp.minimum(glu, SWIGLU_LIMIT)
    lin = jnp.clip(lin, -SWIGLU_LIMIT, SWIGLU_LIMIT)
    act = glu * jax.nn.sigmoid(SWIGLU_ALPHA * glu) * (lin + 1.0)
    y = jnp.dot(act.astype(BF16), wd_ref[0], preferred_element_type=F32) + bd_ref[0]
    y = y * sw_ref[0]

    @pl.when(i >= 2)
    def _():
        wait_scatter(slot)

    ybuf[slot] = y

    def sbody(r, c):
        row = dst_ref[0, 0, r]
        pltpu.make_async_copy(ybuf.at[slot, pl.ds(r, 1)], y_hbm.at[pl.ds(row, 1)], ssem.at[slot]).start()
        return c
    lax.fori_loop(0, g, sbody, 0, unroll=8)

    @pl.when(i == n - 1)
    def _():
        wait_scatter(slot)

        @pl.when(n >= 2)
        def _():
            wait_scatter(1 - slot)


def _moe_experts(h2, block_e, src_tok, dst_row, slot_w, wg, wl, bg, bl, wd, bd):
    t, d = h2.shape
    nb = block_e.shape[0]
    g = EXPERT_BLOCK
    f = wg.shape[2]
    idx_blk = lambda i, be: (i, 0, 0)
    nxt_blk = lambda i, be: (jnp.minimum(i + 1, nb - 1), 0, 0)
    exp_blk = lambda i, be: (be[i], 0, 0)
    grid_spec = pltpu.PrefetchScalarGridSpec(
        num_scalar_prefetch=1,
        grid=(nb,),
        in_specs=[pl.BlockSpec((1, 1, g), idx_blk, memory_space=pltpu.SMEM),
                  pl.BlockSpec((1, 1, g), nxt_blk, memory_space=pltpu.SMEM),
                  pl.BlockSpec((1, 1, g), idx_blk, memory_space=pltpu.SMEM),
                  pl.BlockSpec((1, g, 1), idx_blk),
                  pl.BlockSpec(memory_space=pl.ANY),
                  pl.BlockSpec((1, d, f), exp_blk),
                  pl.BlockSpec((1, d, f), exp_blk),
                  pl.BlockSpec((1, 1, f), exp_blk),
                  pl.BlockSpec((1, 1, f), exp_blk),
                  pl.BlockSpec((1, f, d), exp_blk),
                  pl.BlockSpec((1, 1, d), exp_blk)],
        out_specs=pl.BlockSpec(memory_space=pl.ANY),
        scratch_shapes=[pltpu.VMEM((2, g, d), F32), pltpu.VMEM((2, g, d), F32),
                        pltpu.SemaphoreType.DMA((2,)), pltpu.SemaphoreType.DMA((2,))],
    )
    src3 = src_tok.reshape(nb, 1, g)
    return pl.pallas_call(
        _moe_kernel,
        grid_spec=grid_spec,
        out_shape=jax.ShapeDtypeStruct((nb * g, d), F32),
        compiler_params=_cparams(("arbitrary",)),
        name="moe_experts",
    )(block_e, src3, src3, dst_row.reshape(nb, 1, g), slot_w.reshape(nb, g, 1), h2,
      wg, wl, bg, bl, wd, bd)


def _route(eidx, gates, t):
    a = t * TOP_K
    g = EXPERT_BLOCK
    nb = -(-a // g) + N_EXPERTS
    p = nb * g
    flat_e = eidx.reshape(a)
    flat_w = gates.reshape(a)
    onehot = (flat_e[:, None] == jnp.arange(N_EXPERTS, dtype=jnp.int32)[None, :]).astype(jnp.int32)
    cum = jnp.cumsum(onehot, axis=0)
    rank = jnp.sum(cum * onehot, axis=1) - 1
    counts = cum[-1]
    padded = (counts + g - 1) // g * g
    pend = jnp.cumsum(padded)
    pstart = pend - padded
    slot = pstart[flat_e] + rank
    aidx = jnp.arange(a, dtype=jnp.int32)
    tok = aidx // TOP_K
    dest_a = (aidx % TOP_K) * t + tok
    src_tok = jnp.zeros((p,), jnp.int32).at[slot].set(tok)
    hit = jnp.zeros((p,), jnp.int32).at[slot].set(1)
    pad_ord = jnp.cumsum(1 - hit) - 1
    dst_row = jnp.where(hit == 1, jnp.zeros((p,), jnp.int32).at[slot].set(dest_a), a + pad_ord).astype(jnp.int32)
    slot_w = jnp.zeros((p,), F32).at[slot].set(flat_w)
    block_e = jnp.minimum(jnp.searchsorted(pend, jnp.arange(nb) * g, side='right'), N_EXPERTS - 1).astype(jnp.int32)
    return block_e, src_tok, dst_row, slot_w


def _combine_kernel(x_ref, g2_ref, y0_ref, y1_ref, y2_ref, y3_ref, o_ref):
    o_ref[0] = x_ref[0] + g2_ref[0] * (y0_ref[...] + y1_ref[...] + y2_ref[...] + y3_ref[...])


def _combine(x, g2, ybuf, ts):
    b, s, d = x.shape
    nblk = (b * s) // ts
    spb = s // ts

    def yspec(kk):
        return pl.BlockSpec((ts, d), lambda i, j, kk=kk: (kk * nblk + i * spb + j, 0))

    return pl.pallas_call(
        _combine_kernel,
        grid=(b, spb),
        in_specs=[pl.BlockSpec((1, ts, d), lambda i, j: (i, j, 0)),
                  pl.BlockSpec((1, 1, d), lambda i, j: (i, 0, 0)),
                  yspec(0), yspec(1), yspec(2), yspec(3)],
        out_specs=pl.BlockSpec((1, ts, d), lambda i, j: (i, j, 0)),
        out_shape=jax.ShapeDtypeStruct((b, s, d), F32),
        compiler_params=_cparams(("parallel", "arbitrary")),
        name="moe_combine",
    )(x, g2, ybuf, ybuf, ybuf, ybuf)


def _moe_layer(x_new, h2, eidx, gates, g2, w_gate_up, b_gate_up, w_down, b_down, ts):
    b, s, d = x_new.shape
    t = b * s
    f = w_down.shape[1]
    block_e, src_tok, dst_row, slot_w = _route(eidx[..., :TOP_K], gates[..., :TOP_K], t)
    wg = w_gate_up[:, :, 0::2].astype(BF16)
    wl = w_gate_up[:, :, 1::2].astype(BF16)
    bg = b_gate_up[:, 0::2].reshape(N_EXPERTS, 1, f)
    bl = b_gate_up[:, 1::2].reshape(N_EXPERTS, 1, f)
    ybuf = _moe_experts(h2.reshape(t, d), block_e, src_tok, dst_row, slot_w,
                        wg, wl, bg, bl, w_down.astype(BF16), b_down.reshape(N_EXPERTS, 1, d))
    return _combine(x_new, g2, ybuf, ts)


def _ml_inproj_kernel(x_ref, gain_ref, sc_ref, sh_ref, w_ref, wi_ref, wf_ref,
                      q_ref, k_ref, v_ref, o_ref, gi_ref, gf_ref):
    qk = q_ref.shape[2]
    vw = v_ref.shape[2]
    h = _norm_mod(x_ref[0], gain_ref[...], sc_ref[0], sh_ref[0]).astype(BF16)
    cw = 2 * LANES
    for c in range(qk // cw):
        q_ref[0, :, c * cw:(c + 1) * cw] = jnp.dot(
            h, w_ref[:, c * cw:(c + 1) * cw], preferred_element_type=F32).astype(BF16)
        k_ref[0, :, c * cw:(c + 1) * cw] = (jnp.dot(
            h, w_ref[:, qk + c * cw: qk + (c + 1) * cw], preferred_element_type=F32)
            * (ML_QK_DIM ** -0.5)).astype(BF16)
    for c in range(vw // cw):
        lo = 2 * qk + c * cw
        v_ref[0, :, c * cw:(c + 1) * cw] = jnp.dot(h, w_ref[:, lo:lo + cw], preferred_element_type=F32).astype(BF16)
        lo = 2 * qk + vw + c * cw
        o_ref[0, :, c * cw:(c + 1) * cw] = jax.nn.sigmoid(
            jnp.dot(h, w_ref[:, lo:lo + cw], preferred_element_type=F32)).astype(BF16)
    gi_ref[0] = jnp.dot(h, wi_ref[...], preferred_element_type=F32)
    gf_ref[0] = jnp.dot(h, wf_ref[...], preferred_element_type=F32)


def _ml_inproj(x, gain, sc, sh, w_main, wi_pad, wf_pad, ts):
    b, s, d = x.shape
    qk = ML_HEADS * ML_QK_DIM
    vw = ML_HEADS * ML_V_DIM
    row = lambda i, j: (i, j, 0)
    per_b = lambda i, j: (i, 0, 0)
    const2 = lambda i, j: (0, 0)
    return pl.pallas_call(
        _ml_inproj_kernel,
        grid=(b, s // ts),
        in_specs=[pl.BlockSpec((1, ts, d), row),
                  pl.BlockSpec((1, d), const2),
                  pl.BlockSpec((1, 1, d), per_b),
                  pl.BlockSpec((1, 1, d), per_b),
                  pl.BlockSpec(w_main.shape, const2),
                  pl.BlockSpec(wi_pad.shape, const2),
                  pl.BlockSpec(wf_pad.shape, const2)],
        out_specs=[pl.BlockSpec((1, ts, qk), row), pl.BlockSpec((1, ts, qk), row),
                   pl.BlockSpec((1, ts, vw), row), pl.BlockSpec((1, ts, vw), row),
                   pl.BlockSpec((1, ts, LANES), row), pl.BlockSpec((1, ts, LANES), row)],
        out_shape=[jax.ShapeDtypeStruct((b, s, qk), BF16), jax.ShapeDtypeStruct((b, s, qk), BF16),
                   jax.ShapeDtypeStruct((b, s, vw), BF16), jax.ShapeDtypeStruct((b, s, vw), BF16),
                   jax.ShapeDtypeStruct((b, s, LANES), F32), jax.ShapeDtypeStruct((b, s, LANES), F32)],
        compiler_params=_cparams(("parallel", "arbitrary")),
        name="ml_inproj",
    )(x, gain.reshape(1, d), sc, sh, w_main, wi_pad, wf_pad)


def _softcap(z):
    return GATE_SOFTCAP * jnp.tanh(z / GATE_SOFTCAP)


def _log_sigmoid(z):
    return jnp.minimum(z, 0.0) - jnp.log(1.0 + jnp.exp(-jnp.abs(z)))


def _mlstm_kernel(q_ref, k_ref, v_ref, og_ref, gic_ref, gfc_ref, gir_ref, gfr_ref,
                  bic_ref, bfc_ref, bir_ref, bfr_ref, ng_ref, out_ref, c_sc, n_sc, m_sc, *, lc):
    ic = pl.program_id(1)

    @pl.when(ic == 0)
    def _():
        c_sc[...] = jnp.zeros_like(c_sc)
        n_sc[...] = jnp.zeros_like(n_sc)
        m_sc[...] = jnp.zeros_like(m_sc)

    hi = lax.Precision.HIGHEST
    r = lax.broadcasted_iota(jnp.int32, (lc, lc), 0)
    cidx = lax.broadcasted_iota(jnp.int32, (lc, lc), 1)
    causal = cidx <= r
    tri = causal.astype(F32)
    tri_t = (r <= cidx).astype(F32)

    i_c = _softcap(gic_ref[0] + bic_ref[...])
    lf_c = _log_sigmoid(_softcap(gfc_ref[0] + bfc_ref[...]))
    i_r = _softcap(gir_ref[0] + bir_ref[...])
    lf_r = _log_sigmoid(_softcap(gfr_ref[0] + bfr_ref[...]))
    b_c = jnp.dot(tri, lf_c, preferred_element_type=F32, precision=hi)
    b_r = jnp.dot(lf_r, tri_t, preferred_element_type=F32, precision=hi)
    g_c = b_c[lc - 1:lc, :]
    m_prev = m_sc[...]
    inter_c = b_c + m_prev
    logw_c = g_c - b_c + i_c
    m_new = jnp.maximum(g_c + m_prev, jnp.max(logw_c, axis=0, keepdims=True))
    ws_c = jnp.exp(logw_c - m_new)
    decay = jnp.exp(g_c + m_prev - m_new)
    imb_r = i_r - b_r
    m_sc[...] = m_new

    lane = lax.broadcasted_iota(jnp.int32, (lc, LANES), 1)
    nt = (((1,), (1,)), ((), ()))
    for h in range(ML_HEADS):
        pr = h // 2
        half = h % 2
        qp = q_ref[0, :, pr * LANES:(pr + 1) * LANES]
        kp = k_ref[0, :, pr * LANES:(pr + 1) * LANES]
        mine = (lane >= half * ML_QK_DIM) & (lane < (half + 1) * ML_QK_DIM)
        qh = jnp.where(mine, qp, jnp.zeros_like(qp))
        kh = jnp.where(mine, kp, jnp.zeros_like(kp)).astype(F32)
        vh = v_ref[0, :, h * ML_V_DIM:(h + 1) * ML_V_DIM]

        log_d = jnp.where(causal, b_c[:, h:h + 1] + imb_r[h:h + 1, :], NEG_BIG)
        inter = inter_c[:, h:h + 1]
        m_row = jnp.maximum(inter, jnp.max(log_d, axis=-1, keepdims=True))
        dmat = jnp.exp(log_d - m_row)
        sc = lax.dot_general(qh, kp, nt, preferred_element_type=F32) * dmat
        w_inter = jnp.exp(inter - m_row)
        c_h = c_sc[h]
        n_h = n_sc[h]
        num = (w_inter * jnp.dot(qh, c_h.astype(BF16), preferred_element_type=F32)
               + jnp.dot(sc.astype(BF16), vh, preferred_element_type=F32))
        den = (w_inter * jnp.sum(qh.astype(F32) * n_h, axis=-1, keepdims=True)
               + jnp.sum(sc, axis=-1, keepdims=True))
        ht = num / jnp.maximum(jnp.abs(den), jnp.exp(-m_row))

        kw = kh * ws_c[:, h:h + 1]
        dh = decay[:, h:h + 1]
        c_sc[h] = dh * c_h + jnp.dot(kw.T.astype(BF16), vh, preferred_element_type=F32)
        n_sc[h] = dh * n_h + jnp.sum(kw, axis=0, keepdims=True)

        ms = jnp.mean(ht * ht, axis=-1, keepdims=True)
        hn = ht * lax.rsqrt(ms + RMS_EPS) * ng_ref[:, h * ML_V_DIM:(h + 1) * ML_V_DIM]
        og = og_ref[0, :, h * ML_V_DIM:(h + 1) * ML_V_DIM].astype(F32)
        out_ref[0, :, h * ML_V_DIM:(h + 1) * ML_V_DIM] = (og * hn).astype(BF16)


def _mlstm(q, k, v, og, gi, gf, b_i, b_f, out_gain, lc):
    b, s, vw = v.shape
    qk = q.shape[2]
    nh = ML_HEADS
    gir = jnp.transpose(gi[:, :, :nh], (0, 2, 1))
    gfr = jnp.transpose(gf[:, :, :nh], (0, 2, 1))
    pad = lambda a: jnp.pad(a.astype(F32), (0, LANES - nh)).reshape(1, LANES)
    row = lambda i, j: (i, j, 0)
    rowt = lambda i, j: (i, 0, j)
    const2 = lambda i, j: (0, 0)
    kern = functools.partial(_mlstm_kernel, lc=lc)
    return pl.pallas_call(
        kern,
        grid=(b, s // lc),
        in_specs=[pl.BlockSpec((1, lc, qk), row), pl.BlockSpec((1, lc, qk), row),
                  pl.BlockSpec((1, lc, vw), row), pl.BlockSpec((1, lc, vw), row),
                  pl.BlockSpec((1, lc, LANES), row), pl.BlockSpec((1, lc, LANES), row),
                  pl.BlockSpec((1, nh, lc), rowt), pl.BlockSpec((1, nh, lc), rowt),
                  pl.BlockSpec((1, LANES), const2), pl.BlockSpec((1, LANES), const2),
                  pl.BlockSpec((nh, 1), const2), pl.BlockSpec((nh, 1), const2),
                  pl.BlockSpec((1, vw), const2)],
        out_specs=pl.BlockSpec((1, lc, vw), row),
        out_shape=jax.ShapeDtypeStruct((b, s, vw), BF16),
        scratch_shapes=[pltpu.VMEM((nh, LANES, ML_V_DIM), F32), pltpu.VMEM((nh, 1, LANES), F32),
                        pltpu.VMEM((1, LANES), F32)],
        compiler_params=_cparams(("parallel", "arbitrary")),
        name="mlstm_core",
    )(q, k, v, og, gi, gf, gir, gfr, pad(b_i), pad(b_f),
      b_i.astype(F32).reshape(nh, 1), b_f.astype(F32).reshape(nh, 1), out_gain.reshape(1, vw))


def _rope_tables(positions):
    inv_freq = ROPE_THETA ** (-jnp.arange(0, HEAD_DIM, 2, dtype=F32) / HEAD_DIM)
    ang = positions.astype(F32)[..., None] * inv_freq
    cos, sin = jnp.cos(ang), jnp.sin(ang)
    reps = LANES // (HEAD_DIM // 2)
    cos_t = jnp.tile(cos, (1, 1, reps))
    sin_t = jnp.concatenate([-sin, sin] * (reps // 2), axis=-1)
    return cos_t, sin_t


def _pad_cols(w, n):
    return jnp.pad(w, ((0, 0), (0, n - w.shape[1])))


def kernel(x, c, positions, ada_w, ada_b, mix_norm, ffn_norm, att_w_in, att_w_out, att_q_norm, att_k_norm,
           att_lam_q1, att_lam_k1, att_lam_q2, att_lam_k2, att_sub_norm, ml_w_in, ml_b_igate, ml_b_fgate,
           ml_out_norm, ml_w_out, router_w, router_b, moe_w_gate_up, moe_b_gate_up, moe_w_down, moe_b_down):
    b, s, d = x.shape
    depth = ada_w.shape[0]
    ts = min(512, s)
    tq = min(256, s)
    lc = min(256, s)
    mod = _adaln(c, ada_w, ada_b)
    cos_t, sin_t = _rope_tables(positions)
    for layer in range(depth):
        m6 = mod[layer].reshape(b, 6, 1, d)
        sh1, sc1, g1, sh2, sc2, g2 = (m6[:, i] for i in range(6))
        j = layer // 2
        if layer % 2 == 0:
            lambda_init = 0.8 - 0.6 * math.exp(-0.3 * layer)
            q, k, v = _att_inproj(x, mix_norm[layer], sc1, sh1, att_w_in[j].astype(BF16),
                                  att_q_norm[j], att_k_norm[j], cos_t, sin_t, ts)
            lam_vecs = jnp.stack([att_lam_q1[j], att_lam_k1[j], att_lam_q2[j], att_lam_k2[j]]).astype(F32)
            y = _diff_attention(q, k, v, lam_vecs, att_sub_norm[j].astype(F32), lambda_init, tq)
            w_out = att_w_out[j]
        else:
            qkw = ML_HEADS * ML_QK_DIM
            vw = ML_HEADS * ML_V_DIM
            w_in = ml_w_in[j]
            nm = 2 * qkw + 2 * vw
            q, k, v, og, gi, gf = _ml_inproj(
                x, mix_norm[layer], sc1, sh1, w_in[:, :nm].astype(BF16),
                _pad_cols(w_in[:, nm:nm + ML_HEADS], LANES).astype(BF16),
                _pad_cols(w_in[:, nm + ML_HEADS:], LANES).astype(BF16), ts)
            y = _mlstm(q, k, v, og, gi, gf, ml_b_igate[j], ml_b_fgate[j], ml_out_norm[j].astype(F32), lc)
            w_out = ml_w_out[j]
        rw_pad = _pad_cols(router_w[layer], LANES).astype(BF16)
        rb_pad = jnp.concatenate([router_b[layer].astype(F32),
                                  jnp.full((LANES - N_EXPERTS,), NEG_BIG, F32)]).reshape(1, LANES)
        x_new, h2, eidx, gates = _outproj(y, w_out.astype(BF16), x, g1, ffn_norm[layer], sc2, sh2,
                                          rw_pad, rb_pad, ts)
        x = _moe_layer(x_new, h2, eidx, gates, g2, moe_w_gate_up[layer], moe_b_gate_up[layer],
                       moe_w_down[layer], moe_b_down[layer], ts)
    return x
```

```python
import functools
import math

import jax
import jax.numpy as jnp
from jax import lax
from jax.experimental import pallas as pl
from jax.experimental.pallas import tpu as pltpu

F32 = jnp.float32
BF16 = jnp.bfloat16

HEAD_DIM = 64
ATT_HEADS = 8
ML_HEADS = 8
ML_QK_DIM = 64
ML_V_DIM = 128
ROPE_THETA = 10000.0
GATE_SOFTCAP = 15.0
N_EXPERTS = 32
TOP_K = 4
SWIGLU_ALPHA = 1.702
SWIGLU_LIMIT = 7.0
EXPERT_BLOCK = 512
RMS_EPS = 1e-6
NEG_BIG = -1e30
LANES = 128

VMEM_LIMIT = 56 * 1024 * 1024


def _cparams(sem):
    return pltpu.CompilerParams(dimension_semantics=sem, vmem_limit_bytes=VMEM_LIMIT)


def _norm_mod(x, gain, sc, sh):
    ms = jnp.mean(x * x, axis=-1, keepdims=True)
    return (x * lax.rsqrt(ms + RMS_EPS) * gain) * (1.0 + sc) + sh


def _adaln_kernel(c_ref, w_ref, b_ref, o_ref):
    c = c_ref[...]
    ca = c * jax.nn.sigmoid(c)
    o_ref[0] = jnp.dot(ca, w_ref[0], preferred_element_type=F32,
                       precision=lax.Precision.HIGHEST) + b_ref[0]


def _adaln(c, ada_w, ada_b):
    depth, d, n = ada_w.shape
    b = c.shape[0]
    tn = n // 4
    return pl.pallas_call(
        _adaln_kernel,
        grid=(depth, n // tn),
        in_specs=[pl.BlockSpec((b, d), lambda l, j: (0, 0)),
                  pl.BlockSpec((1, d, tn), lambda l, j: (l, 0, j)),
                  pl.BlockSpec((1, 1, tn), lambda l, j: (l, 0, j))],
        out_specs=pl.BlockSpec((1, b, tn), lambda l, j: (l, 0, j)),
        out_shape=jax.ShapeDtypeStruct((depth, b, n), F32),
        compiler_params=_cparams(("arbitrary", "arbitrary")),
        name="adaln",
    )(c, ada_w, ada_b.reshape(depth, 1, n))


def _att_inproj_kernel(x_ref, gain_ref, sc_ref, sh_ref, w_ref, g64_ref, qg_ref, kg_ref, cos_ref, sin_ref,
                       q_ref, k_ref, v_ref):
    ts = x_ref.shape[1]
    width = q_ref.shape[2]
    h = _norm_mod(x_ref[0], gain_ref[...], sc_ref[0], sh_ref[0]).astype(BF16)
    cos = cos_ref[0]
    sin = sin_ref[0]
    lane = lax.broadcasted_iota(jnp.int32, (ts, LANES), 1)
    first_half = (lane % HEAD_DIM) < (HEAD_DIM // 2)
    cw = 2 * LANES

    def qk_part(col0, g_ref, out_ref):
        for c in range(width // cw):
            z = jnp.dot(h, w_ref[:, col0 + c * cw: col0 + (c + 1) * cw], preferred_element_type=F32)
            msq = jnp.dot((z * z).astype(BF16), g64_ref[...], preferred_element_type=F32)
            zn = z * lax.rsqrt(msq + RMS_EPS) * g_ref[...]
            for s in range(cw // LANES):
                zz = zn[:, s * LANES:(s + 1) * LANES]
                sw = jnp.where(first_half, pltpu.roll(zz, LANES - HEAD_DIM // 2, 1),
                               pltpu.roll(zz, HEAD_DIM // 2, 1))
                lo = c * cw + s * LANES
                out_ref[0, :, lo:lo + LANES] = (zz * cos + sw * sin).astype(BF16)

    qk_part(0, qg_ref, q_ref)
    qk_part(width, kg_ref, k_ref)
    for c in range(width // cw):
        lo = 2 * width + c * cw
        v_ref[0, :, c * cw:(c + 1) * cw] = jnp.dot(
            h, w_ref[:, lo:lo + cw], preferred_element_type=F32).astype(BF16)


def _att_inproj(x, gain, sc, sh, w_bf, q_gain, k_gain, cos_t, sin_t, ts):
    b, s, d = x.shape
    width = w_bf.shape[1] // 3
    cw = 2 * LANES
    gid = jnp.arange(cw) // HEAD_DIM
    g64 = jnp.where(gid[:, None] == gid[None, :], 1.0 / HEAD_DIM, 0.0).astype(BF16)
    reps = cw // HEAD_DIM
    qg = jnp.tile(q_gain.astype(F32) * (HEAD_DIM ** -0.5 * math.log2(math.e)), reps).reshape(1, cw)
    kg = jnp.tile(k_gain.astype(F32), reps).reshape(1, cw)
    row = lambda i, j: (i, j, 0)
    per_b = lambda i, j: (i, 0, 0)
    const2 = lambda i, j: (0, 0)
    out = jax.ShapeDtypeStruct((b, s, width), BF16)
    return pl.pallas_call(
        _att_inproj_kernel,
        grid=(b, s // ts),
        in_specs=[pl.BlockSpec((1, ts, d), row),
                  pl.BlockSpec((1, d), const2),
                  pl.BlockSpec((1, 1, d), per_b),
                  pl.BlockSpec((1, 1, d), per_b),
                  pl.BlockSpec(w_bf.shape, const2),
                  pl.BlockSpec((cw, cw), const2),
                  pl.BlockSpec((1, cw), const2),
                  pl.BlockSpec((1, cw), const2),
                  pl.BlockSpec((1, ts, LANES), row),
                  pl.BlockSpec((1, ts, LANES), row)],
        out_specs=[pl.BlockSpec((1, ts, width), row)] * 3,
        out_shape=[out, out, out],
        compiler_params=_cparams(("parallel", "arbitrary")),
        name="att_inproj",
    )(x, gain.reshape(1, d), sc, sh, w_bf, g64, qg, kg, cos_t, sin_t)


def _attn_kernel(q_ref, k_ref, v_ref, lam_ref, sg_ref, o_ref, m_sc, acc_sc, *, tq, lambda_init):
    iq = pl.program_id(2)
    q = q_ref[0]
    lane = lax.broadcasted_iota(jnp.int32, q.shape, 1)
    zero = jnp.zeros_like(q)
    qs = jnp.concatenate([jnp.where(lane < HEAD_DIM, q, zero), jnp.where(lane >= HEAD_DIM, q, zero)], axis=0)
    nt = (((1,), (1,)), ((), ()))
    ones = jnp.ones((tq, LANES), BF16)
    reps = tq // LANES

    def scores(j):
        off = pl.multiple_of(j * tq, tq)
        kj = k_ref[0, pl.ds(off, tq), :]
        vj = v_ref[0, pl.ds(off, tq), :]
        return lax.dot_general(qs, kj, nt, preferred_element_type=F32), jnp.concatenate([vj, ones], axis=1)

    s, v1 = scores(iq)
    r = lax.broadcasted_iota(jnp.int32, s.shape, 0)
    col = lax.broadcasted_iota(jnp.int32, s.shape, 1)
    rq = jnp.where(r >= tq, r - tq, r)
    s = jnp.where(col <= rq, s, NEG_BIG)
    m0 = jnp.broadcast_to(jnp.max(s, axis=-1, keepdims=True), (2 * tq, LANES))
    p = jnp.exp2(s - jnp.concatenate([m0] * reps, axis=1))
    m_sc[...] = m0
    acc_sc[...] = jnp.dot(p.astype(BF16), v1, preferred_element_type=F32)

    def body(j, carry):
        sj, vj1 = scores(j)
        m_prev = m_sc[...]
        m_new = jnp.maximum(m_prev, jnp.max(sj, axis=-1, keepdims=True))
        alpha = jnp.exp2(m_prev - m_new)
        pj = jnp.exp2(sj - jnp.concatenate([m_new] * reps, axis=1))
        acc_sc[...] = (jnp.concatenate([alpha, alpha], axis=1) * acc_sc[...]
                       + jnp.dot(pj.astype(BF16), vj1, preferred_element_type=F32))
        m_sc[...] = m_new
        return carry

    lax.fori_loop(0, iq, body, 0)

    acc = acc_sc[...]
    lv = lam_ref[...]
    lam = (jnp.exp(jnp.sum(lv[0:1] * lv[1:2], axis=-1, keepdims=True))
           - jnp.exp(jnp.sum(lv[2:3] * lv[3:4], axis=-1, keepdims=True)) + lambda_init)
    o = acc[:tq, :LANES] / acc[:tq, LANES:] - lam * (acc[tq:, :LANES] / acc[tq:, LANES:])
    ms = jnp.mean(o * o, axis=-1, keepdims=True)
    o = o * lax.rsqrt(ms + RMS_EPS) * sg_ref[...] * (1.0 - lambda_init)
    o_ref[0] = o.astype(BF16)


def _diff_attention(q, k, v, lam_vecs, sub_gain, lambda_init, tq):
    b, s, width = q.shape
    nh = width // LANES
    kern = functools.partial(_attn_kernel, tq=tq, lambda_init=lambda_init)
    return pl.pallas_call(
        kern,
        grid=(b, nh, s // tq),
        in_specs=[pl.BlockSpec((1, tq, LANES), lambda i, h, j: (i, j, h)),
                  pl.BlockSpec((1, s, LANES), lambda i, h, j: (i, 0, h)),
                  pl.BlockSpec((1, s, LANES), lambda i, h, j: (i, 0, h)),
                  pl.BlockSpec(lam_vecs.shape, lambda i, h, j: (0, 0)),
                  pl.BlockSpec((1, LANES), lambda i, h, j: (0, 0))],
        out_specs=pl.BlockSpec((1, tq, LANES), lambda i, h, j: (i, j, h)),
        out_shape=jax.ShapeDtypeStruct((b, s, width), BF16),
        scratch_shapes=[pltpu.VMEM((2 * tq, LANES), F32), pltpu.VMEM((2 * tq, 2 * LANES), F32)],
        compiler_params=_cparams(("parallel", "parallel", "arbitrary")),
        name="diff_attn",
    )(q, k, v, lam_vecs, sub_gain.reshape(1, LANES))


def _outproj_kernel(o_ref, w_ref, x_ref, g1_ref, gain_ref, sc_ref, sh_ref, rw_ref, rb_ref,
                    xo_ref, h2_ref, eidx_ref, gate_ref):
    y = jnp.dot(o_ref[0], w_ref[...], preferred_element_type=F32)
    xn = x_ref[0] + g1_ref[0] * y
    xo_ref[0] = xn
    h2 = _norm_mod(xn, gain_ref[...], sc_ref[0], sh_ref[0])
    h2_ref[0] = h2
    logits = jnp.dot(h2.astype(BF16), rw_ref[...], preferred_element_type=F32) + rb_ref[...]
    lane = lax.broadcasted_iota(jnp.int32, logits.shape, 1).astype(F32)
    vals, idxs = [], []
    for _ in range(TOP_K):
        mx = jnp.max(logits, axis=-1, keepdims=True)
        ix = jnp.min(jnp.where(logits == mx, lane, float(LANES)), axis=-1, keepdims=True)
        vals.append(mx)
        idxs.append(ix)
        logits = jnp.where(lane == ix, NEG_BIG * 2.0, logits)
    es = [jnp.exp(vk - vals[0]) for vk in vals]
    den = es[0] + es[1] + es[2] + es[3]
    eo = jnp.zeros(lane.shape, F32)
    go = jnp.zeros(lane.shape, F32)
    for kk in range(TOP_K):
        eo = jnp.where(lane == float(kk), idxs[kk], eo)
        go = jnp.where(lane == float(kk), es[kk] / den, go)
    eidx_ref[0] = eo.astype(jnp.int32)
    gate_ref[0] = go


def _outproj(o, w_bf, x, g1, gain, sc, sh, rw_pad, rb_pad, ts):
    b, s, d = x.shape
    row = lambda i, j: (i, j, 0)
    per_b = lambda i, j: (i, 0, 0)
    const2 = lambda i, j: (0, 0)
    return pl.pallas_call(
        _outproj_kernel,
        grid=(b, s // ts),
        in_specs=[pl.BlockSpec((1, ts, o.shape[2]), row),
                  pl.BlockSpec(w_bf.shape, const2),
                  pl.BlockSpec((1, ts, d), row),
                  pl.BlockSpec((1, 1, d), per_b),
                  pl.BlockSpec((1, d), const2),
                  pl.BlockSpec((1, 1, d), per_b),
                  pl.BlockSpec((1, 1, d), per_b),
                  pl.BlockSpec(rw_pad.shape, const2),
                  pl.BlockSpec((1, LANES), const2)],
        out_specs=[pl.BlockSpec((1, ts, d), row), pl.BlockSpec((1, ts, d), row),
                   pl.BlockSpec((1, ts, LANES), row), pl.BlockSpec((1, ts, LANES), row)],
        out_shape=[jax.ShapeDtypeStruct((b, s, d), F32), jax.ShapeDtypeStruct((b, s, d), F32),
                   jax.ShapeDtypeStruct((b, s, LANES), jnp.int32), jax.ShapeDtypeStruct((b, s, LANES), F32)],
        compiler_params=_cparams(("parallel", "arbitrary")),
        name="outproj_router",
    )(o, w_bf, x, g1, gain.reshape(1, d), sc, sh, rw_pad, rb_pad)


def _moe_kernel(be_ref, src_ref, srcn_ref, dst_ref, h_hbm, wgu_ref, perm_ref, bg_ref, bl_ref, wd_ref, bd_ref,
                y_hbm, xbuf, ybuf, wgl_sc, wd_sc, gsem, ssem):
    i = pl.program_id(0)
    n = pl.num_programs(0)
    g = xbuf.shape[1]
    slot = i % 2

    def issue_gather(idx_ref, sl):
        def body(r, c):
            tok = idx_ref[0, 0, r]
            pltpu.make_async_copy(h_hbm.at[pl.ds(tok, 1)], xbuf.at[sl, pl.ds(r, 1)], gsem.at[sl]).start()
            return c
        lax.fori_loop(0, g, body, 0, unroll=8)

    def wait_gather(sl):
        pltpu.make_async_copy(h_hbm.at[pl.ds(0, g)], xbuf.at[sl], gsem.at[sl]).wait()

    def wait_scatter(sl):
        pltpu.make_async_copy(ybuf.at[sl], y_hbm.at[pl.ds(0, g)], ssem.at[sl]).wait()

    @pl.when(i == 0)
    def _():
        issue_gather(src_ref, 0)

    @pl.when(i + 1 < n)
    def _():
        issue_gather(srcn_ref, 1 - slot)

    f = wd_sc.shape[0]
    cw = perm_ref.shape[0]

    @pl.when((i == 0) | (be_ref[i] != be_ref[jnp.maximum(i - 1, 0)]))
    def _():
        for c in range(2 * f // cw):
            chunk = wgu_ref[0, :, c * cw:(c + 1) * cw].astype(BF16)
            sep = jnp.dot(chunk, perm_ref[...], preferred_element_type=F32).astype(BF16)
            hw = cw // 2
            wgl_sc[:, c * hw:(c + 1) * hw] = sep[:, :hw]
            wgl_sc[:, f + c * hw: f + (c + 1) * hw] = sep[:, hw:]
        wd_sc[...] = wd_ref[0].astype(BF16)

    wait_gather(slot)
    x = xbuf[slot].astype(BF16)
    gu = jnp.dot(x, wgl_sc[...], preferred_element_type=F32)
    glu = jnp.minimum(gu[:, :f] + bg_ref[0], SWIGLU_LIMIT)
    lin = jnp.clip(gu[:, f:] + bl_ref[0], -SWIGLU_LIMIT, SWIGLU_LIMIT)
    act = glu * jax.nn.sigmoid(SWIGLU_ALPHA * glu) * (lin + 1.0)
    y = jnp.dot(act.astype(BF16), wd_sc[...], preferred_element_type=F32) + bd_ref[0]

    @pl.when(i >= 2)
    def _():
        wait_scatter(slot)

    ybuf[slot] = y

    def sbody(r, c):
        row = dst_ref[0, 0, r]
        pltpu.make_async_copy(ybuf.at[slot, pl.ds(r, 1)], y_hbm.at[pl.ds(row, 1)], ssem.at[slot]).start()
        return c
    lax.fori_loop(0, g, sbody, 0, unroll=8)

    @pl.when(i == n - 1)
    def _():
        wait_scatter(slot)

        @pl.when(n >= 2)
        def _():
            wait_scatter(1 - slot)


def _moe_experts(h2, block_e, src_tok, dst_row, w_gate_up, bg, bl, w_down, bd):
    t, d = h2.shape
    nb = block_e.shape[0]
    g = EXPERT_BLOCK
    f = w_down.shape[1]
    cw = 2 * LANES
    cidx = jnp.arange(cw)
    perm = (jnp.where(cidx % 2 == 0, cidx // 2, cw // 2 + cidx // 2)[:, None] == cidx[None, :]).astype(BF16)
    idx_blk = lambda i, be: (i, 0, 0)
    nxt_blk = lambda i, be: (jnp.minimum(i + 1, nb - 1), 0, 0)
    exp_blk = lambda i, be: (be[i], 0, 0)
    grid_spec = pltpu.PrefetchScalarGridSpec(
        num_scalar_prefetch=1,
        grid=(nb,),
        in_specs=[pl.BlockSpec((1, 1, g), idx_blk, memory_space=pltpu.SMEM),
                  pl.BlockSpec((1, 1, g), nxt_blk, memory_space=pltpu.SMEM),
                  pl.BlockSpec((1, 1, g), idx_blk, memory_space=pltpu.SMEM),
                  pl.BlockSpec(memory_space=pl.ANY),
                  pl.BlockSpec((1, d, 2 * f), exp_blk),
                  pl.BlockSpec((cw, cw), lambda i, be: (0, 0)),
                  pl.BlockSpec((1, 1, f), exp_blk),
                  pl.BlockSpec((1, 1, f), exp_blk),
                  pl.BlockSpec((1, f, d), exp_blk),
                  pl.BlockSpec((1, 1, d), exp_blk)],
        out_specs=pl.BlockSpec(memory_space=pl.ANY),
        scratch_shapes=[pltpu.VMEM((2, g, d), F32), pltpu.VMEM((2, g, d), F32),
                        pltpu.VMEM((d, 2 * f), BF16), pltpu.VMEM((f, d), BF16),
                        pltpu.SemaphoreType.DMA((2,)), pltpu.SemaphoreType.DMA((2,))],
    )
    src3 = src_tok.reshape(nb, 1, g)
    return pl.pallas_call(
        _moe_kernel,
        grid_spec=grid_spec,
        out_shape=jax.ShapeDtypeStruct((nb * g, d), F32),
        compiler_params=_cparams(("arbitrary",)),
        name="moe_experts",
    )(block_e, src3, src3, dst_row.reshape(nb, 1, g), h2, w_gate_up, perm, bg, bl, w_down, bd)


def _route(eidx, t):
    a = t * TOP_K
    g = EXPERT_BLOCK
    nb = -(-a // g) + N_EXPERTS
    p = nb * g
    flat_e = eidx.reshape(a)
    onehot = (flat_e[:, None] == jnp.arange(N_EXPERTS, dtype=jnp.int32)[None, :]).astype(jnp.int32)
    cum = jnp.cumsum(onehot, axis=0)
    rank = jnp.sum(cum * onehot, axis=1) - 1
    counts = cum[-1]
    padded = (counts + g - 1) // g * g
    pend = jnp.cumsum(padded)
    pstart = pend - padded
    slot = pstart[flat_e] + rank
    inv = jnp.full((p,), -1, jnp.int32).at[slot].set(jnp.arange(a, dtype=jnp.int32))
    hit = inv >= 0
    tok = jnp.maximum(inv, 0) // TOP_K
    pad_ord = jnp.cumsum(jnp.where(hit, 0, 1)) - 1
    dst_row = jnp.where(hit, (inv % TOP_K) * t + tok, a + pad_ord).astype(jnp.int32)
    block_e = jnp.minimum(jnp.searchsorted(pend, jnp.arange(nb) * g, side='right'), N_EXPERTS - 1).astype(jnp.int32)
    return block_e, tok.astype(jnp.int32), dst_row


def _combine_kernel(x_ref, g2_ref, gate_ref, y0_ref, y1_ref, y2_ref, y3_ref, o_ref):
    gt = gate_ref[0]
    mix = (gt[:, 0:1] * y0_ref[...] + gt[:, 1:2] * y1_ref[...]
           + gt[:, 2:3] * y2_ref[...] + gt[:, 3:4] * y3_ref[...])
    o_ref[0] = x_ref[0] + g2_ref[0] * mix


def _combine(x, g2, gates, ybuf, ts):
    b, s, d = x.shape
    nblk = (b * s) // ts
    spb = s // ts

    def yspec(kk):
        return pl.BlockSpec((ts, d), lambda i, j, kk=kk: (kk * nblk + i * spb + j, 0))

    return pl.pallas_call(
        _combine_kernel,
        grid=(b, spb),
        in_specs=[pl.BlockSpec((1, ts, d), lambda i, j: (i, j, 0)),
                  pl.BlockSpec((1, 1, d), lambda i, j: (i, 0, 0)),
                  pl.BlockSpec((1, ts, LANES), lambda i, j: (i, j, 0)),
                  yspec(0), yspec(1), yspec(2), yspec(3)],
        out_specs=pl.BlockSpec((1, ts, d), lambda i, j: (i, j, 0)),
        out_shape=jax.ShapeDtypeStruct((b, s, d), F32),
        compiler_params=_cparams(("parallel", "arbitrary")),
        name="moe_combine",
    )(x, g2, gates, ybuf, ybuf, ybuf, ybuf)


def _moe_layer(x_new, h2, eidx, gates, g2, w_gate_up, b_gate_up, w_down, b_down, ts):
    b, s, d = x_new.shape
    t = b * s
    f = w_down.shape[1]
    block_e, src_tok, dst_row = _route(eidx[..., :TOP_K], t)
    bg = b_gate_up[:, 0::2].reshape(N_EXPERTS, 1, f)
    bl = b_gate_up[:, 1::2].reshape(N_EXPERTS, 1, f)
    ybuf = _moe_experts(h2.reshape(t, d), block_e, src_tok, dst_row, w_gate_up, bg, bl,
                        w_down, b_down.reshape(N_EXPERTS, 1, d))
    return _combine(x_new, g2, gates, ybuf, ts)


def _ml_inproj_kernel(x_ref, gain_ref, sc_ref, sh_ref, w_ref, wi_ref, wf_ref,
                      q_ref, k_ref, v_ref, o_ref, gi_ref, gf_ref):
    qk = q_ref.shape[2]
    vw = v_ref.shape[2]
    h = _norm_mod(x_ref[0], gain_ref[...], sc_ref[0], sh_ref[0]).astype(BF16)
    cw = 2 * LANES
    for c in range(qk // cw):
        q_ref[0, :, c * cw:(c + 1) * cw] = jnp.dot(
            h, w_ref[:, c * cw:(c + 1) * cw], preferred_element_type=F32).astype(BF16)
        k_ref[0, :, c * cw:(c + 1) * cw] = (jnp.dot(
            h, w_ref[:, qk + c * cw: qk + (c + 1) * cw], preferred_element_type=F32)
            * (ML_QK_DIM ** -0.5)).astype(BF16)
    for c in range(vw // cw):
        lo = 2 * qk + c * cw
        v_ref[0, :, c * cw:(c + 1) * cw] = jnp.dot(h, w_ref[:, lo:lo + cw], preferred_element_type=F32).astype(BF16)
        lo = 2 * qk + vw + c * cw
        o_ref[0, :, c * cw:(c + 1) * cw] = jax.nn.sigmoid(
            jnp.dot(h, w_ref[:, lo:lo + cw], preferred_element_type=F32)).astype(BF16)
    gi_ref[0] = jnp.dot(h, wi_ref[...], preferred_element_type=F32)
    gf_ref[0] = jnp.dot(h, wf_ref[...], preferred_element_type=F32)


def _ml_inproj(x, gain, sc, sh, w_main, wi_pad, wf_pad, ts):
    b, s, d = x.shape
    qk = ML_HEADS * ML_QK_DIM
    vw = ML_HEADS * ML_V_DIM
    row = lambda i, j: (i, j, 0)
    per_b = lambda i, j: (i, 0, 0)
    const2 = lambda i, j: (0, 0)
    return pl.pallas_call(
        _ml_inproj_kernel,
        grid=(b, s // ts),
        in_specs=[pl.BlockSpec((1, ts, d), row),
                  pl.BlockSpec((1, d), const2),
                  pl.BlockSpec((1, 1, d), per_b),
                  pl.BlockSpec((1, 1, d), per_b),
                  pl.BlockSpec(w_main.shape, const2),
                  pl.BlockSpec(wi_pad.shape, const2),
                  pl.BlockSpec(wf_pad.shape, const2)],
        out_specs=[pl.BlockSpec((1, ts, qk), row), pl.BlockSpec((1, ts, qk), row),
                   pl.BlockSpec((1, ts, vw), row), pl.BlockSpec((1, ts, vw), row),
                   pl.BlockSpec((1, ts, LANES), row), pl.BlockSpec((1, ts, LANES), row)],
        out_shape=[jax.ShapeDtypeStruct((b, s, qk), BF16), jax.ShapeDtypeStruct((b, s, qk), BF16),
                   jax.ShapeDtypeStruct((b, s, vw), BF16), jax.ShapeDtypeStruct((b, s, vw), BF16),
                   jax.ShapeDtypeStruct((b, s, LANES), F32), jax.ShapeDtypeStruct((b, s, LANES), F32)],
        compiler_params=_cparams(("parallel", "arbitrary")),
        name="ml_inproj",
    )(x, gain.reshape(1, d), sc, sh, w_main, wi_pad, wf_pad)


def _softcap(z):
    return GATE_SOFTCAP * jnp.tanh(z / GATE_SOFTCAP)


def _log_sigmoid(z):
    return jnp.minimum(z, 0.0) - jnp.log(1.0 + jnp.exp(-jnp.abs(z)))


def _mlstm_kernel(q_ref, k_ref, v_ref, og_ref, gic_ref, gfc_ref, gir_ref, gfr_ref,
                  bic_ref, bfc_ref, bir_ref, bfr_ref, ng_ref, out_ref, c_sc, n_sc, m_sc, *, lc):
    ic = pl.program_id(1)

    @pl.when(ic == 0)
    def _():
        c_sc[...] = jnp.zeros_like(c_sc)
        n_sc[...] = jnp.zeros_like(n_sc)
        m_sc[...] = jnp.zeros_like(m_sc)

    hi = lax.Precision.HIGHEST
    r = lax.broadcasted_iota(jnp.int32, (lc, lc), 0)
    cidx = lax.broadcasted_iota(jnp.int32, (lc, lc), 1)
    causal = cidx <= r
    tri = causal.astype(F32)
    tri_t = (r <= cidx).astype(F32)

    i_c = _softcap(gic_ref[0] + bic_ref[...])
    lf_c = _log_sigmoid(_softcap(gfc_ref[0] + bfc_ref[...]))
    i_r = _softcap(gir_ref[0] + bir_ref[...])
    lf_r = _log_sigmoid(_softcap(gfr_ref[0] + bfr_ref[...]))
    b_c = jnp.dot(tri, lf_c, preferred_element_type=F32, precision=hi)
    b_r = jnp.dot(lf_r, tri_t, preferred_element_type=F32, precision=hi)
    g_c = b_c[lc - 1:lc, :]
    m_prev = m_sc[...]
    inter_c = b_c + m_prev
    logw_c = g_c - b_c + i_c
    m_new = jnp.maximum(g_c + m_prev, jnp.max(logw_c, axis=0, keepdims=True))
    ws_c = jnp.exp(logw_c - m_new)
    decay = jnp.exp(g_c + m_prev - m_new)
    imb_r = i_r - b_r
    m_sc[...] = m_new

    lane = lax.broadcasted_iota(jnp.int32, (lc, LANES), 1)
    nt = (((1,), (1,)), ((), ()))
    for h in range(ML_HEADS):
        pr = h // 2
        half = h % 2
        qp = q_ref[0, :, pr * LANES:(pr + 1) * LANES]
        kp = k_ref[0, :, pr * LANES:(pr + 1) * LANES]
        mine = (lane >= half * ML_QK_DIM) & (lane < (half + 1) * ML_QK_DIM)
        qh = jnp.where(mine, qp, jnp.zeros_like(qp))
        kh = jnp.where(mine, kp, jnp.zeros_like(kp)).astype(F32)
        vh = v_ref[0, :, h * ML_V_DIM:(h + 1) * ML_V_DIM]

        log_d = jnp.where(causal, b_c[:, h:h + 1] + imb_r[h:h + 1, :], NEG_BIG)
        inter = inter_c[:, h:h + 1]
        m_row = jnp.maximum(inter, jnp.max(log_d, axis=-1, keepdims=True))
        dmat = jnp.exp(log_d - m_row)
        sc = lax.dot_general(qh, kp, nt, preferred_element_type=F32) * dmat
        w_inter = jnp.exp(inter - m_row)
        c_h = c_sc[h]
        n_h = n_sc[h]
        num = (w_inter * jnp.dot(qh, c_h.astype(BF16), preferred_element_type=F32)
               + jnp.dot(sc.astype(BF16), vh, preferred_element_type=F32))
        den = (w_inter * jnp.sum(qh.astype(F32) * n_h, axis=-1, keepdims=True)
               + jnp.sum(sc, axis=-1, keepdims=True))
        ht = num / jnp.maximum(jnp.abs(den), jnp.exp(-m_row))

        kw = kh * ws_c[:, h:h + 1]
        dh = decay[:, h:h + 1]
        c_sc[h] = dh * c_h + jnp.dot(kw.T.astype(BF16), vh, preferred_element_type=F32)
        n_sc[h] = dh * n_h + jnp.sum(kw, axis=0, keepdims=True)

        ms = jnp.mean(ht * ht, axis=-1, keepdims=True)
        hn = ht * lax.rsqrt(ms + RMS_EPS) * ng_ref[:, h * ML_V_DIM:(h + 1) * ML_V_DIM]
        og = og_ref[0, :, h * ML_V_DIM:(h + 1) * ML_V_DIM].astype(F32)
        out_ref[0, :, h * ML_V_DIM:(h + 1) * ML_V_DIM] = (og * hn).astype(BF16)


def _mlstm(q, k, v, og, gi, gf, b_i, b_f, out_gain, lc):
    b, s, vw = v.shape
    qk = q.shape[2]
    nh = ML_HEADS
    gir = jnp.transpose(gi[:, :, :nh], (0, 2, 1))
    gfr = jnp.transpose(gf[:, :, :nh], (0, 2, 1))
    pad = lambda a: jnp.pad(a.astype(F32), (0, LANES - nh)).reshape(1, LANES)
    row = lambda i, j: (i, j, 0)
    rowt = lambda i, j: (i, 0, j)
    const2 = lambda i, j: (0, 0)
    kern = functools.partial(_mlstm_kernel, lc=lc)
    return pl.pallas_call(
        kern,
        grid=(b, s // lc),
        in_specs=[pl.BlockSpec((1, lc, qk), row), pl.BlockSpec((1, lc, qk), row),
                  pl.BlockSpec((1, lc, vw), row), pl.BlockSpec((1, lc, vw), row),
                  pl.BlockSpec((1, lc, LANES), row), pl.BlockSpec((1, lc, LANES), row),
                  pl.BlockSpec((1, nh, lc), rowt), pl.BlockSpec((1, nh, lc), rowt),
                  pl.BlockSpec((1, LANES), const2), pl.BlockSpec((1, LANES), const2),
                  pl.BlockSpec((nh, 1), const2), pl.BlockSpec((nh, 1), const2),
                  pl.BlockSpec((1, vw), const2)],
        out_specs=pl.BlockSpec((1, lc, vw), row),
        out_shape=jax.ShapeDtypeStruct((b, s, vw), BF16),
        scratch_shapes=[pltpu.VMEM((nh, LANES, ML_V_DIM), F32), pltpu.VMEM((nh, 1, LANES), F32),
                        pltpu.VMEM((1, LANES), F32)],
        compiler_params=_cparams(("parallel", "arbitrary")),
        name="mlstm_core",
    )(q, k, v, og, gi, gf, gir, gfr, pad(b_i), pad(b_f),
      b_i.astype(F32).reshape(nh, 1), b_f.astype(F32).reshape(nh, 1), out_gain.reshape(1, vw))


def _rope_tables(positions):
    inv_freq = ROPE_THETA ** (-jnp.arange(0, HEAD_DIM, 2, dtype=F32) / HEAD_DIM)
    ang = positions.astype(F32)[..., None] * inv_freq
    cos, sin = jnp.cos(ang), jnp.sin(ang)
    reps = LANES // (HEAD_DIM // 2)
    cos_t = jnp.tile(cos, (1, 1, reps))
    sin_t = jnp.concatenate([-sin, sin] * (reps // 2), axis=-1)
    return cos_t, sin_t


def _pad_cols(w, n):
    return jnp.pad(w, ((0, 0), (0, n - w.shape[1])))


def kernel(x, c, positions, ada_w, ada_b, mix_norm, ffn_norm, att_w_in, att_w_out, att_q_norm, att_k_norm,
           att_lam_q1, att_lam_k1, att_lam_q2, att_lam_k2, att_sub_norm, ml_w_in, ml_b_igate, ml_b_fgate,
           ml_out_norm, ml_w_out, router_w, router_b, moe_w_gate_up, moe_b_gate_up, moe_w_down, moe_b_down):
    b, s, d = x.shape
    depth = ada_w.shape[0]
    ts = min(512, s)
    tq = min(512, s)
    lc = min(256, s)
    mod = _adaln(c, ada_w, ada_b)
    cos_t, sin_t = _rope_tables(positions)
    for layer in range(depth):
        m6 = mod[layer].reshape(b, 6, 1, d)
        sh1, sc1, g1, sh2, sc2, g2 = (m6[:, i] for i in range(6))
        j = layer // 2
        if layer % 2 == 0:
            lambda_init = 0.8 - 0.6 * math.exp(-0.3 * layer)
            q, k, v = _att_inproj(x, mix_norm[layer], sc1, sh1, att_w_in[j].astype(BF16),
                                  att_q_norm[j], att_k_norm[j], cos_t, sin_t, ts)
            lam_vecs = jnp.stack([att_lam_q1[j], att_lam_k1[j], att_lam_q2[j], att_lam_k2[j]]).astype(F32)
            y = _diff_attention(q, k, v, lam_vecs, att_sub_norm[j].astype(F32), lambda_init, tq)
            w_out = att_w_out[j]
        else:
            qkw = ML_HEADS * ML_QK_DIM
            vw = ML_HEADS * ML_V_DIM
            w_in = ml_w_in[j]
            nm = 2 * qkw + 2 * vw
            q, k, v, og, gi, gf = _ml_inproj(
                x, mix_norm[layer], sc1, sh1, w_in[:, :nm].astype(BF16),
                _pad_cols(w_in[:, nm:nm + ML_HEADS], LANES).astype(BF16),
                _pad_cols(w_in[:, nm + ML_HEADS:], LANES).astype(BF16), ts)
            y = _mlstm(q, k, v, og, gi, gf, ml_b_igate[j], ml_b_fgate[j], ml_out_norm[j].astype(F32), lc)
            w_out = ml_w_out[j]
        rw_pad = _pad_cols(router_w[layer], LANES).astype(BF16)
        rb_pad = jnp.concatenate([router_b[layer].astype(F32),
                                  jnp.full((LANES - N_EXPERTS,), NEG_BIG, F32)]).reshape(1, LANES)
        x_new, h2, eidx, gates = _outproj(y, w_out.astype(BF16), x, g1, ffn_norm[layer], sc2, sh2,
                                          rw_pad, rb_pad, ts)
        x = _moe_layer(x_new, h2, eidx, gates, g2, moe_w_gate_up[layer], moe_b_gate_up[layer],
                       moe_w_down[layer], moe_b_down[layer], ts)
    return x
```
